```python
import jax, jax.numpy as jnp
from jax import lax
import numpy as np

D_MODEL = 1024
BATCH = 16
SEQ = 256
DEPTH = 4
DEC_BATCH = 8
DEC_SEQ = 1024
PAST_LEN = 512

GRID_W = 64
N_MIXERS = 3
N_POOL_LAYERS = (DEPTH + 2) // 3
N_ATTN_LAYERS = (DEPTH + 1) // 3
N_FOURIER_LAYERS = DEPTH // 3
POOL_WINDOWS = (2, 4, 8, 16)
POOL_GROUPS = 4
POOL_CH = D_MODEL // POOL_GROUPS
HEAD_DIM = 64
N_HEADS = D_MODEL // HEAD_DIM
WIN_ROWS = 8
WIN_COLS = 16
FOURIER_GROUPS = 4
N_EXPERTS = 16
N_EXPERT_GROUPS = 4
EXPERTS_PER_GROUP = N_EXPERTS // N_EXPERT_GROUPS
TOP_K = 2
D_EXPERT = 512
EPS = 1e-6
ATTN_SCALE = HEAD_DIM ** -0.5

kernel_name = 'hybrid_pool_natten_fourier_moe_dit_step'


def rms_norm(x, g):
    xf = x.astype(jnp.float32)
    y = xf * lax.rsqrt(jnp.mean(xf * xf, axis=-1, keepdims=True) + EPS) * g.astype(jnp.float32)
    return y.astype(x.dtype)


def pool_mix(h, w, scale):
    B, N, D = h.shape
    hf = h.astype(jnp.float32)
    cs = jnp.concatenate([jnp.zeros((B, 1, D), jnp.float32), jnp.cumsum(hf, axis=1)], axis=1)
    t = jnp.arange(N)
    outs = []
    for g, win in enumerate(POOL_WINDOWS):
        lo = jnp.clip(t - win // 2, 0, N)
        hi = jnp.clip(t - win // 2 + win, 0, N)
        sl = slice(g * POOL_CH, (g + 1) * POOL_CH)
        mean = (cs[:, hi, sl] - cs[:, lo, sl]) / (hi - lo).astype(jnp.float32)[None, :, None]
        outs.append(mean - hf[:, :, sl])
    p = jnp.stack(outs, axis=2).astype(h.dtype)
    y = jnp.einsum('bngc,gcd->bngd', p, w).reshape(B, N, D)
    return y * scale


def fourier_mix(h, w):
    B, N, D = h.shape
    hg = h.astype(jnp.float32).reshape(B, N, FOURIER_GROUPS, D // FOURIER_GROUPS)
    f = jnp.fft.fft2(hg, axes=(1, 3), norm='ortho').real.astype(h.dtype).reshape(B, N, D)
    return f @ w


def na_context(h, wqkv, wo):
    B, N, D = h.shape
    qkv = (h @ wqkv).reshape(B, N, 3, N_HEADS, HEAD_DIM)
    q, k, v = qkv[:, :, 0], qkv[:, :, 1], qkv[:, :, 2]
    s = jnp.einsum('bqhd,bkhd->bhqk', q, k).astype(jnp.float32) * ATTN_SCALE
    p = jax.nn.softmax(s, axis=-1).astype(v.dtype)
    o = jnp.einsum('bhqk,bkhd->bqhd', p, v).reshape(B, N, D)
    return o @ wo, k, v


def na_latent(h, ctx_k, ctx_v, wqkv, wo, rpb):
    B, N, D = h.shape
    R = N // GRID_W
    KR = min(WIN_ROWS, R)
    qkv = (h @ wqkv).reshape(B, R, GRID_W, 3, N_HEADS, HEAD_DIM)
    q, k, v = qkv[:, :, :, 0], qkv[:, :, :, 1], qkv[:, :, :, 2]
    r = jnp.arange(R)
    rs = jnp.clip(r - KR // 2, 0, R - KR)
    row_idx = rs[:, None] + jnp.arange(KR)[None, :]
    kg = k[:, row_idx]
    vg = v[:, row_idx]
    cq = jnp.arange(GRID_W)
    cstart = jnp.clip(cq - WIN_COLS // 2, 0, GRID_W - WIN_COLS)
    ck = jnp.arange(GRID_W)
    valid_col = (ck[None, :] >= cstart[:, None]) & (ck[None, :] < cstart[:, None] + WIN_COLS)
    row_b = (row_idx - r[:, None]) + (WIN_ROWS - 1)
    col_b = jnp.clip(ck[None, :] - cq[:, None] + (WIN_COLS - 1), 0, 2 * WIN_COLS - 2)
    bias = rpb[:, row_b[:, None, :, None], col_b[None, :, None, :]].astype(jnp.float32)
    s_loc = jnp.einsum('brqhd,brkwhd->bhrqkw', q, kg).astype(jnp.float32) * ATTN_SCALE + bias[None]
    s_loc = jnp.where(valid_col[:, None, :], s_loc, -jnp.inf)
    s_ctx = jnp.einsum('brqhd,bmhd->bhrqm', q, ctx_k).astype(jnp.float32) * ATTN_SCALE
    n_loc = KR * GRID_W
    s = jnp.concatenate([s_loc.reshape(B, N_HEADS, R, GRID_W, n_loc), s_ctx], axis=-1)
    p = jax.nn.softmax(s, axis=-1).astype(v.dtype)
    p_loc = p[..., :n_loc].reshape(B, N_HEADS, R, GRID_W, KR, GRID_W)
    p_ctx = p[..., n_loc:]
    o = (jnp.einsum('bhrqkw,brkwhd->brqhd', p_loc, vg)
         + jnp.einsum('bhrqm,bmhd->brqhd', p_ctx, ctx_v)).reshape(B, N, D)
    return o @ wo


def moe(h, router_w, router_b, w_gate, w_up, w_down):
    scores = jax.nn.sigmoid((h @ router_w).astype(jnp.float32))
    sel = scores + router_b.astype(jnp.float32)
    sel_g = sel.reshape(sel.shape[:-1] + (N_EXPERT_GROUPS, EXPERTS_PER_GROUP))
    group_score = lax.top_k(sel_g, 2)[0].sum(-1)
    best = jnp.argmax(group_score, axis=-1)
    in_group = (jnp.arange(N_EXPERTS) // EXPERTS_PER_GROUP) == best[..., None]
    _, idx = lax.top_k(jnp.where(in_group, sel, -jnp.inf), TOP_K)
    wts = jnp.take_along_axis(scores, idx, axis=-1)
    wts = wts / jnp.sum(wts, axis=-1, keepdims=True)
    gates = jnp.sum(jax.nn.one_hot(idx, N_EXPERTS, dtype=jnp.float32) * wts[..., None], axis=-2)
    g = jnp.einsum('bnd,edf->bnef', h, w_gate)
    u = jnp.einsum('bnd,edf->bnef', h, w_up)
    a = jax.nn.silu(g) * u * gates[..., None].astype(h.dtype)
    return jnp.einsum('bnef,efd->bnd', a, w_down)


def trunk(x, cond, ctx_k, ctx_v, ada_w, ada_b, norm1_g, norm2_g, pool_w, pool_scale,
          attn_wqkv, attn_wo, attn_rpb, fourier_w, router_w, router_b,
          moe_w_gate, moe_w_up, moe_w_down, final_norm_g):
    D = x.shape[-1]
    ks, vs = [], []
    cond_act = jax.nn.silu(cond)
    for i in range(DEPTH):
        mod = (cond_act @ ada_w[i] + ada_b[i])[:, None, :]
        sh1, sc1, g1, sh2, sc2, g2 = [mod[..., n * D:(n + 1) * D] for n in range(6)]
        h = rms_norm(x, norm1_g[i]) * (1 + sc1) + sh1
        kind, j = i % N_MIXERS, i // N_MIXERS
        if kind == 0:
            m = pool_mix(h, pool_w[j], pool_scale[j])
        elif kind == 1:
            if ctx_k is None:
                m, k, v = na_context(h, attn_wqkv[j], attn_wo[j])
                ks.append(k)
                vs.append(v)
            else:
                m = na_latent(h, ctx_k[:, j], ctx_v[:, j], attn_wqkv[j], attn_wo[j], attn_rpb[j])
        else:
            m = fourier_mix(h, fourier_w[j])
        x = x + g1 * m
        h = rms_norm(x, norm2_g[i]) * (1 + sc2) + sh2
        x = x + g2 * moe(h, router_w, router_b, moe_w_gate[i], moe_w_up[i], moe_w_down[i])
    return rms_norm(x, final_norm_g), ks, vs


def setup_inputs(seed: int = 0) -> dict:
    key = jax.random.key(seed)
    kk = jax.random.split(key, 24)
    D = D_MODEL

    def nrm(k, shape, s):
        return jax.random.normal(k, shape, jnp.float32) * s

    return {
        'x_prompt': nrm(kk[0], (BATCH, SEQ, D), 1.0),
        'x_sample': nrm(kk[1], (DEC_BATCH, DEC_SEQ, D), 1.0),
        'cache_k': nrm(kk[2], (DEC_BATCH, N_ATTN_LAYERS, PAST_LEN, N_HEADS, HEAD_DIM), 1.0),
        'cache_v': nrm(kk[3], (DEC_BATCH, N_ATTN_LAYERS, PAST_LEN, N_HEADS, HEAD_DIM), 1.0),
        'c': nrm(kk[4], (DEC_BATCH, D), 1.0),
        'c_ctx': nrm(kk[5], (D,), 1.0),
        'ada_w': nrm(kk[6], (DEPTH, D, 6 * D), 0.5 * D ** -0.5),
        'ada_b': nrm(kk[7], (DEPTH, 6 * D), 0.01),
        'norm1_g': 1.0 + nrm(kk[8], (DEPTH, D), 0.1),
        'norm2_g': 1.0 + nrm(kk[9], (DEPTH, D), 0.1),
        'pool_w': nrm(kk[10], (N_POOL_LAYERS, POOL_GROUPS, POOL_CH, POOL_CH), POOL_CH ** -0.5),
        'pool_scale': 1.0 + nrm(kk[11], (N_POOL_LAYERS, D), 0.1),
        'attn_wqkv': nrm(kk[12], (N_ATTN_LAYERS, D, 3 * D), D ** -0.5),
        'attn_wo': nrm(kk[13], (N_ATTN_LAYERS, D, D), D ** -0.5),
        'attn_rpb': nrm(kk[14], (N_ATTN_LAYERS, N_HEADS, 2 * WIN_ROWS - 1, 2 * WIN_COLS - 1), 0.1),
        'fourier_w': nrm(kk[15], (N_FOURIER_LAYERS, D, D), D ** -0.5),
        'router_w': nrm(kk[16], (D, N_EXPERTS), D ** -0.5),
        'router_b': nrm(kk[17], (N_EXPERTS,), 0.01),
        'moe_w_gate': nrm(kk[18], (DEPTH, N_EXPERTS, D, D_EXPERT), D ** -0.5),
        'moe_w_up': nrm(kk[19], (DEPTH, N_EXPERTS, D, D_EXPERT), D ** -0.5),
        'moe_w_down': nrm(kk[20], (DEPTH, N_EXPERTS, D_EXPERT, D), D_EXPERT ** -0.5),
        'final_norm_g': 1.0 + nrm(kk[21], (D,), 0.1),
    }


def reference(x_prompt, x_sample, cache_k, cache_v, c, c_ctx, ada_w, ada_b, norm1_g, norm2_g,
              pool_w, pool_scale, attn_wqkv, attn_wo, attn_rpb, fourier_w, router_w, router_b,
              moe_w_gate, moe_w_up, moe_w_down, final_norm_g):
    y_prompt, ks, vs = trunk(x_prompt, c_ctx[None, :], None, None, ada_w, ada_b, norm1_g, norm2_g,
                             pool_w, pool_scale, attn_wqkv, attn_wo, attn_rpb, fourier_w,
                             router_w, router_b, moe_w_gate, moe_w_up, moe_w_down, final_norm_g)
    new_cache_k = jnp.stack(ks, axis=1)
    new_cache_v = jnp.stack(vs, axis=1)
    y_sample, _, _ = trunk(x_sample, c, cache_k, cache_v, ada_w, ada_b, norm1_g, norm2_g,
                           pool_w, pool_scale, attn_wqkv, attn_wo, attn_rpb, fourier_w,
                           router_w, router_b, moe_w_gate, moe_w_up, moe_w_down, final_norm_g)
    return (y_prompt, y_sample, new_cache_k, new_cache_v)
```

```python
import functools
import math

import numpy as np
import jax
import jax.numpy as jnp
from jax import lax
from jax.experimental import pallas as pl
from jax.experimental.pallas import tpu as pltpu

D = 1024
BATCH = 16
SEQ = 256
DEPTH = 4
DEC_BATCH = 8
DEC_SEQ = 1024
PAST_LEN = 512
GRID_W = 64
GRID_R = DEC_SEQ // GRID_W
POOL_WINDOWS = (2, 4, 8, 16)
POOL_CH = D // len(POOL_WINDOWS)
HEAD_DIM = 64
N_HEADS = D // HEAD_DIM
WIN_ROWS = 8
WIN_COLS = 16
FOURIER_GROUPS = 4
FOURIER_CH = D // FOURIER_GROUPS
N_EXPERTS = 16
N_EXPERT_GROUPS = 4
EXPERTS_PER_GROUP = N_EXPERTS // N_EXPERT_GROUPS
D_EXPERT = 512
EPS = 1e-6
ATTN_SCALE = HEAD_DIM ** -0.5

T_CTX = BATCH * SEQ
T_LAT = DEC_BATCH * DEC_SEQ
T_ALL = T_CTX + T_LAT
ROW_BLK = 256
N_ROW_BLKS = T_ALL // ROW_BLK
CTX_BLKS = T_CTX // ROW_BLK
N_ASSIGN = 2 * T_ALL
HALF = D // 2
PACKED = jnp.uint32
MOE_TILE = 256
N_MOE_TILES = N_ASSIGN // MOE_TILE
N_MOE_ITEMS = N_MOE_TILES + N_EXPERTS - 1
RANK_CHUNK = 512
DMA_CHUNK = 512
NEG = -1e30
MOD_ROWS = 16

VMEM_LIMIT = 56 * 1024 * 1024

f32 = jnp.float32
bf16 = jnp.bfloat16


def _cparams(*sem):
    return pltpu.CompilerParams(dimension_semantics=sem, vmem_limit_bytes=VMEM_LIMIT)


def _silu(x):
    return x * (1.0 / (1.0 + jnp.exp(-x)))


def _sigmoid(x):
    return 1.0 / (1.0 + jnp.exp(-x))


def _norm_mod(x, g, shift, scale):
    ms = jnp.mean(x * x, axis=-1, keepdims=True)
    return x * lax.rsqrt(ms + EPS) * g * (1.0 + scale) + shift


def _dot(a, b):
    return jnp.dot(a, b, preferred_element_type=f32)


def _dot_nt(a, b):
    return lax.dot_general(a, b, (((1,), (1,)), ((), ())), preferred_element_type=f32)


def _mod_row_of_block(j):
    return jnp.where(j < CTX_BLKS, 0, 1 + (j - CTX_BLKS) // (DEC_SEQ // ROW_BLK))


def _mod_spec_rows(layer):
    return pl.BlockSpec((None, None, 1, 6 * D), lambda j: (layer, _mod_row_of_block(j), 0, 0))


def _ada_kernel(c_ref, w_ref, b_ref, o_ref):
    a = _silu(c_ref[...]).astype(bf16)
    o_ref[...] = _dot(a, w_ref[...].astype(bf16)) + b_ref[...]


def _ada(cond16, ada_w, ada_b):
    tn = 2048
    return pl.pallas_call(
        _ada_kernel,
        grid=(DEPTH, 6 * D // tn),
        in_specs=[
            pl.BlockSpec((MOD_ROWS, D), lambda l, n: (0, 0)),
            pl.BlockSpec((None, D, tn), lambda l, n: (l, 0, n)),
            pl.BlockSpec((None, 1, tn), lambda l, n: (l, 0, n)),
        ],
        out_specs=pl.BlockSpec((None, MOD_ROWS, tn), lambda l, n: (l, 0, n)),
        out_shape=jax.ShapeDtypeStruct((DEPTH, MOD_ROWS, 6 * D), f32),
        compiler_params=_cparams("arbitrary", "arbitrary"),
        name="ada_mod",
    )(cond16, ada_w, ada_b.reshape(DEPTH, 1, 6 * D))


SEQ_BLK = DEC_SEQ
CTX_SEQ_BLKS = T_CTX // SEQ_BLK
N_SEQ_BLKS = T_ALL // SEQ_BLK


def _seq_mixer_call(kernel, name, x_all, mod4, layer, consts):
    def whole(a):
        return pl.BlockSpec(a.shape, lambda s, nd=a.ndim: (0,) * nd)

    return pl.pallas_call(
        kernel,
        grid=(N_SEQ_BLKS,),
        in_specs=[
            pl.BlockSpec((SEQ_BLK, D), lambda s: (s, 0)),
            pl.BlockSpec((None, None, 1, 6 * D),
                         lambda s: (layer, jnp.maximum(s - (CTX_SEQ_BLKS - 1), 0), 0, 0)),
        ] + [whole(a) for a in consts],
        out_specs=pl.BlockSpec((SEQ_BLK, D), lambda s: (s, 0)),
        out_shape=jax.ShapeDtypeStruct((T_ALL, D), f32),
        compiler_params=_cparams("arbitrary"),
        name=name,
    )(x_all, mod4, *consts)


def _per_sequence(x_ref, o_ref, fn):
    is_ctx = pl.program_id(0) < CTX_SEQ_BLKS

    @pl.when(is_ctx)
    def _ctx():
        for q in range(SEQ_BLK // SEQ):
            rows = slice(q * SEQ, (q + 1) * SEQ)
            o_ref[rows, :] = fn(x_ref[rows, :], SEQ)

    @pl.when(jnp.logical_not(is_ctx))
    def _lat():
        o_ref[...] = fn(x_ref[...], DEC_SEQ)


def _pool_kernel(x_ref, mod_ref, g_ref, w_ref, scale_ref, o_ref):
    mod = mod_ref[...]

    def mix(x, n):
        h = _norm_mod(x, g_ref[...], mod[:, 0:D], mod[:, D:2 * D])
        hb = h.astype(bf16)
        d = lax.broadcasted_iota(jnp.int32, (n, n), 1) - lax.broadcasted_iota(jnp.int32, (n, n), 0)
        t = lax.broadcasted_iota(jnp.int32, (n, 1), 0)
        outs = []
        for gi, win in enumerate(POOL_WINDOWS):
            half = win // 2
            band = ((d >= -half) & (d < win - half)).astype(bf16)
            cnt = (jnp.minimum(t - half + win, n) - jnp.maximum(t - half, 0)).astype(f32)
            sl = slice(gi * POOL_CH, (gi + 1) * POOL_CH)
            mean = _dot(band, hb[:, sl]) / cnt
            p = (mean - h[:, sl]).astype(bf16)
            outs.append(_dot(p, w_ref[gi]))
        y = jnp.concatenate(outs, axis=1) * scale_ref[...]
        return x + mod[:, 2 * D:3 * D] * y

    _per_sequence(x_ref, o_ref, mix)


def _dft_mats(n):
    k = np.arange(n, dtype=np.int64)
    ang = 2.0 * np.pi * ((k[:, None] * k[None, :]) % n).astype(np.float64) / n
    return np.cos(ang), np.sin(ang)


def _fourier_kernel(x_ref, mod_ref, g_ref, csc_ref, cs_ctx_ref, cs_lat_ref, w_ref, o_ref):
    mod = mod_ref[...]

    def mix(x, n):
        cs_ref = cs_ctx_ref if n == SEQ else cs_lat_ref
        hb = _norm_mod(x, g_ref[...], mod[:, 0:D], mod[:, D:2 * D]).astype(bf16)
        ys, zs = [], []
        for gi in range(FOURIER_GROUPS):
            yz = _dot(hb[:, gi * FOURIER_CH:(gi + 1) * FOURIER_CH], csc_ref[...])
            ys.append(yz[:, :FOURIER_CH])
            zs.append(yz[:, FOURIER_CH:])
        y = jnp.concatenate(ys, axis=1).astype(bf16)
        z = jnp.concatenate(zs, axis=1).astype(bf16)
        f = (_dot(cs_ref[0], y) + _dot(cs_ref[1], z)) * (1.0 / math.sqrt(n * FOURIER_CH))
        return x + mod[:, 2 * D:3 * D] * _dot(f.astype(bf16), w_ref[...])

    _per_sequence(x_ref, o_ref, mix)


def _fourier_consts():
    cc, sc = _dft_mats(FOURIER_CH)
    csc = jnp.asarray(np.concatenate([cc, sc], axis=1), f32).astype(bf16)
    cs = []
    for n in (SEQ, DEC_SEQ):
        cn, sn = _dft_mats(n)
        cs.append(jnp.asarray(np.stack([cn, -sn]), f32).astype(bf16))
    return csc, cs[0], cs[1]


def _qkv_kernel(x_ref, mod_ref, g_ref, w_ref, q_ref, k_ref, v_ref):
    mod = mod_ref[...]
    hb = _norm_mod(x_ref[...], g_ref[...], mod[:, 0:D], mod[:, D:2 * D]).astype(bf16)
    q_ref[...] = _dot(hb, w_ref[:, 0:D]).astype(q_ref.dtype)
    k_ref[...] = _dot(hb, w_ref[:, D:2 * D]).astype(k_ref.dtype)
    v_ref[...] = _dot(hb, w_ref[:, 2 * D:3 * D]).astype(v_ref.dtype)


def _qkv_call(x_all, mod4, layer, g, wqkv, *, ctx):
    nblk = CTX_BLKS if ctx else N_ROW_BLKS - CTX_BLKS
    blk0 = 0 if ctx else CTX_BLKS
    rows = nblk * ROW_BLK
    kv_dtype = f32 if ctx else bf16
    out = pl.BlockSpec((ROW_BLK, D), lambda j: (j, 0))
    return pl.pallas_call(
        _qkv_kernel,
        grid=(nblk,),
        in_specs=[
            pl.BlockSpec((ROW_BLK, D), lambda j: (blk0 + j, 0)),
            pl.BlockSpec((None, None, 1, 6 * D),
                         lambda j: (layer, _mod_row_of_block(blk0 + j), 0, 0)),
            pl.BlockSpec((1, D), lambda j: (0, 0)),
            pl.BlockSpec((D, 3 * D), lambda j: (0, 0)),
        ],
        out_specs=[out, out, out],
        out_shape=[jax.ShapeDtypeStruct((rows, D), bf16),
                   jax.ShapeDtypeStruct((rows, D), kv_dtype),
                   jax.ShapeDtypeStruct((rows, D), kv_dtype)],
        compiler_params=_cparams("arbitrary"),
        name="qkv_ctx" if ctx else "qkv_lat",
    )(x_all, mod4, g, wqkv)


def _attn_ctx_kernel(q_ref, k_ref, v_ref, o_ref):
    q = q_ref[...]
    k = k_ref[...].astype(bf16)
    v = v_ref[...].astype(bf16)
    outs = []
    for h in range(2):
        sl = slice(h * HEAD_DIM, (h + 1) * HEAD_DIM)
        s = _dot_nt(q[:, sl], k[:, sl]) * ATTN_SCALE
        p = jnp.exp(s - jnp.max(s, axis=-1, keepdims=True))
        l = jnp.sum(p, axis=-1, keepdims=True)
        outs.append(_dot(p.astype(bf16), v[:, sl]) / l)
    o_ref[...] = jnp.concatenate(outs, axis=1).astype(o_ref.dtype)


def _attn_ctx_call(q, k, v):
    spec = pl.BlockSpec((SEQ, 2 * HEAD_DIM), lambda b, hp: (b, hp))
    return pl.pallas_call(
        _attn_ctx_kernel,
        grid=(BATCH, N_HEADS // 2),
        in_specs=[spec, spec, spec],
        out_specs=spec,
        out_shape=jax.ShapeDtypeStruct((T_CTX, D), bf16),
        compiler_params=_cparams("arbitrary", "arbitrary"),
        name="attn_ctx",
    )(q, k, v)


def _row_start(r):
    return min(max(r - WIN_ROWS // 2, 0), GRID_R - WIN_ROWS)


Q_ROWS = 4


def _key_range(c):
    lo = _row_start(c * Q_ROWS)
    hi = _row_start(c * Q_ROWS + Q_ROWS - 1) + WIN_ROWS
    lo = (lo // 4) * 4
    hi = -(-hi // 4) * 4
    return lo * GRID_W, hi * GRID_W


def _attn_lat_kernel(q_ref, k_ref, v_ref, ck_ref, cv_ref, tb_ref, o_ref, bias_ref):
    @pl.when(pl.program_id(1) == 0)
    def _build_bias():
        lane = lax.broadcasted_iota(jnp.int32, (GRID_W, 2 * GRID_W), 1)
        neg = jnp.full((GRID_W, 2 * GRID_W), NEG, f32)
        for h in range(2):
            for qr in range(GRID_R):
                rs = _row_start(qr)
                for p in range(GRID_R // 2):
                    halves = []
                    for kr in (2 * p, 2 * p + 1):
                        ok = rs <= kr < rs + WIN_ROWS
                        halves.append(tb_ref[h, kr - qr + WIN_ROWS - 1] if ok else neg)
                    piece = jnp.where(lane < GRID_W, halves[0], halves[1])
                    bias_ref[h, qr * GRID_W:(qr + 1) * GRID_W,
                             p * 2 * GRID_W:(p + 1) * 2 * GRID_W] = piece

    q = q_ref[...]
    k = k_ref[...]
    v = v_ref[...]
    ck = ck_ref[...].astype(bf16)
    cv = cv_ref[...].astype(bf16)
    qn = Q_ROWS * GRID_W
    for c in range(GRID_R // Q_ROWS):
        k0, k1 = _key_range(c)
        outs = []
        for h in range(2):
            sl = slice(h * HEAD_DIM, (h + 1) * HEAD_DIM)
            kh, vh, ckh, cvh = k[:, sl], v[:, sl], ck[:, sl], cv[:, sl]
            qc = q[c * qn:(c + 1) * qn, sl]
            s_loc = _dot_nt(qc, kh[k0:k1]) * ATTN_SCALE + bias_ref[h, c * qn:(c + 1) * qn, k0:k1]
            s_ctx = _dot_nt(qc, ckh) * ATTN_SCALE
            m = jnp.maximum(jnp.max(s_loc, axis=-1, keepdims=True),
                            jnp.max(s_ctx, axis=-1, keepdims=True))
            p_loc = jnp.exp(s_loc - m)
            p_ctx = jnp.exp(s_ctx - m)
            l = jnp.sum(p_loc, axis=-1, keepdims=True) + jnp.sum(p_ctx, axis=-1, keepdims=True)
            outs.append((_dot(p_loc.astype(bf16), vh[k0:k1]) + _dot(p_ctx.astype(bf16), cvh)) / l)
        o_ref[c * qn:(c + 1) * qn, :] = jnp.concatenate(outs, axis=1).astype(o_ref.dtype)


def _attn_lat_call(q, k, v, cache_k, cache_v, tbl):
    spec = pl.BlockSpec((DEC_SEQ, 2 * HEAD_DIM), lambda hp, b: (b, hp))
    cspec = pl.BlockSpec((None, PAST_LEN, 2 * HEAD_DIM), lambda hp, b: (b, 0, hp))
    return pl.pallas_call(
        _attn_lat_kernel,
        grid=(N_HEADS // 2, DEC_BATCH),
        in_specs=[spec, spec, spec, cspec, cspec,
                  pl.BlockSpec((2, 2 * WIN_ROWS - 1, GRID_W, 2 * GRID_W), lambda hp, b: (hp, 0, 0, 0))],
        out_specs=spec,
        out_shape=jax.ShapeDtypeStruct((T_LAT, D), bf16),
        scratch_shapes=[pltpu.VMEM((2, DEC_SEQ, DEC_SEQ), f32)],
        compiler_params=_cparams("arbitrary", "arbitrary"),
        name="attn_lat",
    )(q, k, v, cache_k, cache_v, tbl)


def _rel_bias_table(rpb):
    cq = jnp.arange(GRID_W)
    cstart = jnp.clip(cq - WIN_COLS // 2, 0, GRID_W - WIN_COLS)
    ck = jnp.arange(GRID_W)
    valid = (ck[None, :] >= cstart[:, None]) & (ck[None, :] < cstart[:, None] + WIN_COLS)
    col_b = jnp.clip(ck[None, :] - cq[:, None] + (WIN_COLS - 1), 0, 2 * WIN_COLS - 2)
    tb = jnp.where(valid[None, None], rpb[:, :, col_b].astype(f32), NEG)
    return jnp.concatenate([tb, tb], axis=-1)


def _proj_kernel(o_ref, x_ref, mod_ref, w_ref, out_ref):
    out_ref[...] = x_ref[...] + mod_ref[:, 2 * D:3 * D] * _dot(o_ref[...], w_ref[...])


def _proj_call(o_all, x_all, mod4, layer, wo):
    row = pl.BlockSpec((ROW_BLK, D), lambda j: (j, 0))
    return pl.pallas_call(
        _proj_kernel,
        grid=(N_ROW_BLKS,),
        in_specs=[row, row, _mod_spec_rows(layer), pl.BlockSpec((D, D), lambda j: (0, 0))],
        out_specs=row,
        out_shape=jax.ShapeDtypeStruct((T_ALL, D), f32),
        compiler_params=_cparams("arbitrary"),
        name="attn_proj",
    )(o_all, x_all, mod4, wo)


def _split3(x):
    a = x.astype(bf16)
    r = x - a.astype(f32)
    b = r.astype(bf16)
    c = (r - b.astype(f32)).astype(bf16)
    return a, b, c


def _pack_rows(h):
    return pltpu.pack_elementwise([h[:, :HALF], h[:, HALF:]], packed_dtype=bf16)


def _unpack_rows(w):
    halves = [pltpu.unpack_elementwise(w, index=i, packed_dtype=bf16, unpacked_dtype=f32)
              for i in range(2)]
    return jnp.concatenate(halves, axis=1)


def _router_kernel(x_ref, mod_ref, g_ref, rwt_ref, rb_ref, hp_ref, e_ref, w_ref, cnt_ref):
    mod = mod_ref[...]
    h = _norm_mod(x_ref[...], g_ref[...], mod[:, 3 * D:4 * D], mod[:, 4 * D:5 * D])
    hp_ref[...] = _pack_rows(h)

    h1, h2, h3 = _split3(h)
    r1, r2, r3 = _split3(rwt_ref[...])
    logit = (_dot_nt(r1, h1) + (_dot_nt(r1, h2) + _dot_nt(r2, h1))
             + (_dot_nt(r1, h3) + _dot_nt(r2, h2) + _dot_nt(r3, h1)))
    score = _sigmoid(logit)
    sel = score + rb_ref[...]

    def row(a, e):
        return a[e:e + 1, :]

    best_gs, best_g = None, None
    for gidx in range(N_EXPERT_GROUPS):
        v = [row(sel, gidx * EXPERTS_PER_GROUP + j) for j in range(EXPERTS_PER_GROUP)]
        gs = None
        for a in range(EXPERTS_PER_GROUP):
            for b in range(a + 1, EXPERTS_PER_GROUP):
                s = v[a] + v[b]
                gs = s if gs is None else jnp.maximum(gs, s)
        if best_gs is None:
            best_gs, best_g = gs, jnp.zeros_like(gs, dtype=jnp.int32)
        else:
            upd = gs > best_gs
            best_g = jnp.where(upd, gidx, best_g)
            best_gs = jnp.where(upd, gs, best_gs)

    def pick(a, j):
        out = row(a, j)
        for gidx in range(1, N_EXPERT_GROUPS):
            out = jnp.where(best_g == gidx, row(a, gidx * EXPERTS_PER_GROUP + j), out)
        return out

    vs = [pick(sel, j) for j in range(EXPERTS_PER_GROUP)]
    ss = [pick(score, j) for j in range(EXPERTS_PER_GROUP)]
    i0, v0, s0 = jnp.zeros_like(best_g), vs[0], ss[0]
    for j in range(1, EXPERTS_PER_GROUP):
        upd = vs[j] > v0
        i0 = jnp.where(upd, j, i0)
        v0 = jnp.where(upd, vs[j], v0)
        s0 = jnp.where(upd, ss[j], s0)
    i1, v1, s1 = None, None, None
    for j in range(EXPERTS_PER_GROUP):
        cand = jnp.where(i0 == j, -jnp.inf, vs[j])
        if i1 is None:
            i1, v1, s1 = jnp.zeros_like(best_g), cand, ss[0]
        else:
            upd = cand > v1
            i1 = jnp.where(upd, j, i1)
            v1 = jnp.where(upd, cand, v1)
            s1 = jnp.where(upd, ss[j], s1)
    e0 = best_g * EXPERTS_PER_GROUP + i0
    e1 = best_g * EXPERTS_PER_GROUP + i1
    tot = s0 + s1
    e_ref[0:1, :] = e0
    e_ref[1:2, :] = e1
    w_ref[0:1, :] = s0 / tot
    w_ref[1:2, :] = s1 / tot

    eid = lax.broadcasted_iota(jnp.int32, (N_EXPERTS, ROW_BLK), 0)
    hits = (eid == e0).astype(f32) + (eid == e1).astype(f32)
    c = jnp.broadcast_to(jnp.sum(hits, axis=1, keepdims=True), (N_EXPERTS, 128))

    @pl.when(pl.program_id(0) == 0)
    def _init():
        cnt_ref[...] = jnp.zeros_like(cnt_ref)

    cnt_ref[...] += c


def _router_call(x_all, mod4, layer, g, rwt, rb):
    return pl.pallas_call(
        _router_kernel,
        grid=(N_ROW_BLKS,),
        in_specs=[
            pl.BlockSpec((ROW_BLK, D), lambda j: (j, 0)),
            _mod_spec_rows(layer),
            pl.BlockSpec((1, D), lambda j: (0, 0)),
            pl.BlockSpec((N_EXPERTS, D), lambda j: (0, 0)),
            pl.BlockSpec((N_EXPERTS, 1), lambda j: (0, 0)),
        ],
        out_specs=[
            pl.BlockSpec((ROW_BLK, HALF), lambda j: (j, 0)),
            pl.BlockSpec((2, ROW_BLK), lambda j: (0, j)),
            pl.BlockSpec((2, ROW_BLK), lambda j: (0, j)),
            pl.BlockSpec((N_EXPERTS, 128), lambda j: (0, 0)),
        ],
        out_shape=[
            jax.ShapeDtypeStruct((T_ALL, HALF), PACKED),
            jax.ShapeDtypeStruct((2, T_ALL), jnp.int32),
            jax.ShapeDtypeStruct((2, T_ALL), f32),
            jax.ShapeDtypeStruct((N_EXPERTS, 128), f32),
        ],
        compiler_params=_cparams("arbitrary"),
        name="moe_router",
    )(x_all, mod4, g, rwt, rb)


def _rank_kernel(e_ref, off_ref, pos_ref, carry_ref):
    @pl.when(pl.program_id(0) == 0)
    def _init():
        carry_ref[...] = jnp.zeros_like(carry_ref)

    e = e_ref[...]
    eid = lax.broadcasted_iota(jnp.int32, (N_EXPERTS, RANK_CHUNK), 0)
    onehot = eid == e
    s = lax.broadcasted_iota(jnp.int32, (RANK_CHUNK, RANK_CHUNK), 0)
    t = lax.broadcasted_iota(jnp.int32, (RANK_CHUNK, RANK_CHUNK), 1)
    upper = (s <= t).astype(bf16)
    cum = _dot(onehot.astype(bf16), upper)
    base = off_ref[...] + carry_ref[...]
    pos = jnp.sum(jnp.where(onehot, cum - 1.0 + base, 0.0), axis=0, keepdims=True)
    pos_ref[...] = pos.astype(jnp.int32)
    carry_ref[...] += cum[:, RANK_CHUNK - 1:RANK_CHUNK]


def _rank_call(e_flat, off_col):
    return pl.pallas_call(
        _rank_kernel,
        grid=(N_ASSIGN // RANK_CHUNK,),
        in_specs=[pl.BlockSpec((1, RANK_CHUNK), lambda i: (0, i)),
                  pl.BlockSpec((N_EXPERTS, 1), lambda i: (0, 0))],
        out_specs=pl.BlockSpec((1, RANK_CHUNK), lambda i: (0, i)),
        out_shape=jax.ShapeDtypeStruct((1, N_ASSIGN), jnp.int32),
        scratch_shapes=[pltpu.VMEM((N_EXPERTS, 1), f32)],
        compiler_params=_cparams("arbitrary"),
        name="moe_rank",
    )(e_flat, off_col)


def _row_copy(src_hbm, dst_hbm, sem, s, d):
    return pltpu.make_async_copy(src_hbm.at[pl.ds(s, 1)], dst_hbm.at[pl.ds(d, 1)], sem)


def _dispatch_kernel(pos_ref, h_hbm, xs_hbm, sem):
    base = (pl.program_id(0) * DMA_CHUNK) % T_ALL

    def start(r, c):
        _row_copy(h_hbm, xs_hbm, sem, base + r, pos_ref[0, r]).start()
        return c

    def wait(r, c):
        _row_copy(h_hbm, xs_hbm, sem, 0, 0).wait()
        return c

    lax.fori_loop(0, DMA_CHUNK, start, 0)
    lax.fori_loop(0, DMA_CHUNK, wait, 0)


def _undispatch_kernel(pos_ref, ys_hbm, yg_hbm, sem):
    base = pl.program_id(0) * DMA_CHUNK

    def start(r, c):
        _row_copy(ys_hbm, yg_hbm, sem, pos_ref[0, r], base + r).start()
        return c

    def wait(r, c):
        _row_copy(ys_hbm, yg_hbm, sem, 0, 0).wait()
        return c

    lax.fori_loop(0, DMA_CHUNK, start, 0)
    lax.fori_loop(0, DMA_CHUNK, wait, 0)


def _row_move_call(kernel, pos3, src, name):
    return pl.pallas_call(
        kernel,
        grid=(N_ASSIGN // DMA_CHUNK,),
        in_specs=[pl.BlockSpec((None, 1, DMA_CHUNK), lambda i: (i, 0, 0), memory_space=pltpu.SMEM),
                  pl.BlockSpec(memory_space=pl.ANY)],
        out_specs=pl.BlockSpec(memory_space=pl.ANY),
        out_shape=jax.ShapeDtypeStruct((N_ASSIGN, HALF), PACKED),
        scratch_shapes=[pltpu.SemaphoreType.DMA(())],
        compiler_params=_cparams("arbitrary"),
        name=name,
    )(pos3, src)


def _experts_kernel(tile_ref, exp_ref, lo_ref, hi_ref, xs_ref, wg_ref, wu_ref, wd_ref, ys_ref):
    i = pl.program_id(0)
    lo = lo_ref[i]
    hi = hi_ref[i]
    tile0 = tile_ref[i] * MOE_TILE

    @pl.when(hi > lo)
    def _work():
        x = _unpack_rows(xs_ref[...]).astype(bf16)
        g = _dot(x, wg_ref[...].astype(bf16))
        u = _dot(x, wu_ref[...].astype(bf16))
        a = (_silu(g) * u).astype(bf16)
        y = _dot(a, wd_ref[...].astype(bf16))
        slot = tile0 + lax.broadcasted_iota(jnp.int32, (MOE_TILE, 1), 0)
        mine = (slot >= lo) & (slot < hi)

        @pl.when(lo == tile0)
        def _first():
            ys_ref[...] = _pack_rows(jnp.where(mine, y, 0.0))

        @pl.when(lo != tile0)
        def _rest():
            ys_ref[...] = _pack_rows(jnp.where(mine, y, _unpack_rows(ys_ref[...])))


def _experts_call(item_tile, item_exp, item_lo, item_hi, xs, wg, wu, wd, layer):
    rows = pl.BlockSpec((MOE_TILE, HALF), lambda i, t, e, lo, hi: (t[i], 0))
    return pl.pallas_call(
        _experts_kernel,
        grid_spec=pltpu.PrefetchScalarGridSpec(
            num_scalar_prefetch=4,
            grid=(N_MOE_ITEMS,),
            in_specs=[
                rows,
                pl.BlockSpec((None, None, D, D_EXPERT), lambda i, t, e, lo, hi: (layer, e[i], 0, 0)),
                pl.BlockSpec((None, None, D, D_EXPERT), lambda i, t, e, lo, hi: (layer, e[i], 0, 0)),
                pl.BlockSpec((None, None, D_EXPERT, D), lambda i, t, e, lo, hi: (layer, e[i], 0, 0)),
            ],
            out_specs=rows,
        ),
        out_shape=jax.ShapeDtypeStruct((N_ASSIGN, HALF), PACKED),
        compiler_params=_cparams("arbitrary"),
        name="moe_experts",
    )(item_tile, item_exp, item_lo, item_hi, xs, wg, wu, wd)


def _moe_items(counts):
    off_end = jnp.cumsum(counts)
    off = off_end - counts
    tile_lo = jnp.arange(N_MOE_TILES, dtype=jnp.int32) * MOE_TILE
    bounds = jnp.sort(jnp.concatenate([tile_lo, off_end[:-1]]))
    lo = bounds
    hi = jnp.concatenate([bounds[1:], jnp.array([N_ASSIGN], jnp.int32)])
    tile = jnp.minimum(lo // MOE_TILE, N_MOE_TILES - 1)
    exp = jnp.minimum(jnp.sum((off_end[None, :] <= lo[:, None]).astype(jnp.int32), axis=1), N_EXPERTS - 1)
    return off, tile.astype(jnp.int32), exp.astype(jnp.int32), lo.astype(jnp.int32), hi.astype(jnp.int32)


def _combine_kernel(x_ref, mod_ref, y0_ref, y1_ref, w_ref, *rest, final):
    if final:
        gf_ref, o_ref = rest
    else:
        (o_ref,) = rest
    w = w_ref[...]
    y = w[:, 0:1] * _unpack_rows(y0_ref[...]) + w[:, 1:2] * _unpack_rows(y1_ref[...])
    x = x_ref[...] + mod_ref[:, 5 * D:6 * D] * y
    if final:
        ms = jnp.mean(x * x, axis=-1, keepdims=True)
        x = x * lax.rsqrt(ms + EPS) * gf_ref[...]
    o_ref[...] = x


def _combine_call(x_all, mod4, layer, yg, w_t, *, final_g=None, blk0=0, nblk=N_ROW_BLKS):
    final = final_g is not None
    in_specs = [
        pl.BlockSpec((ROW_BLK, D), lambda j: (blk0 + j, 0)),
        pl.BlockSpec((None, None, 1, 6 * D), lambda j: (layer, _mod_row_of_block(blk0 + j), 0, 0)),
        pl.BlockSpec((ROW_BLK, HALF), lambda j: (blk0 + j, 0)),
        pl.BlockSpec((ROW_BLK, HALF), lambda j: (N_ROW_BLKS + blk0 + j, 0)),
        pl.BlockSpec((ROW_BLK, 2), lambda j: (blk0 + j, 0)),
    ]
    args = [x_all, mod4, yg, yg, w_t]
    if final:
        in_specs.append(pl.BlockSpec((1, D), lambda j: (0, 0)))
        args.append(final_g)
    return pl.pallas_call(
        functools.partial(_combine_kernel, final=final),
        grid=(nblk,),
        in_specs=in_specs,
        out_specs=pl.BlockSpec((ROW_BLK, D), lambda j: (j, 0)),
        out_shape=jax.ShapeDtypeStruct((nblk * ROW_BLK, D), f32),
        compiler_params=_cparams("arbitrary"),
        name="moe_combine",
    )(*args)


def _moe_layer(x_all, mod4, layer, g2, rwt, rb, wg, wu, wd, final_g):
    hp, e, w, cnt = _router_call(x_all, mod4, layer, g2, rwt, rb)
    counts = cnt[:, 0].astype(jnp.int32)
    off, item_tile, item_exp, item_lo, item_hi = _moe_items(counts)
    pos = _rank_call(e.reshape(1, N_ASSIGN), off.astype(f32).reshape(N_EXPERTS, 1))
    pos3 = pos.reshape(N_ASSIGN // DMA_CHUNK, 1, DMA_CHUNK)
    xs = _row_move_call(_dispatch_kernel, pos3, hp, "moe_dispatch")
    ys = _experts_call(item_tile, item_exp, item_lo, item_hi, xs, wg, wu, wd, layer)
    yg = _row_move_call(_undispatch_kernel, pos3, ys, "moe_undispatch")
    w_t = w.T
    if final_g is None:
        return _combine_call(x_all, mod4, layer, yg, w_t)
    y_c = _combine_call(x_all, mod4, layer, yg, w_t, final_g=final_g, blk0=0, nblk=CTX_BLKS)
    y_l = _combine_call(x_all, mod4, layer, yg, w_t, final_g=final_g, blk0=CTX_BLKS,
                        nblk=N_ROW_BLKS - CTX_BLKS)
    return y_c, y_l


def kernel(x_prompt, x_sample, cache_k, cache_v, c, c_ctx, ada_w, ada_b, norm1_g, norm2_g, pool_w,
           pool_scale, attn_wqkv, attn_wo, attn_rpb, fourier_w, router_w, router_b, moe_w_gate,
           moe_w_up, moe_w_down, final_norm_g):
    x_all = jnp.concatenate([x_prompt.reshape(T_CTX, D), x_sample.reshape(T_LAT, D)], axis=0)
    cond16 = jnp.concatenate(
        [c_ctx[None, :], c, jnp.zeros((MOD_ROWS - 1 - DEC_BATCH, D), f32)], axis=0)
    mod4 = _ada(cond16, ada_w, ada_b).reshape(DEPTH, MOD_ROWS, 1, 6 * D)
    rwt = router_w.T
    rb = router_b.reshape(N_EXPERTS, 1)
    new_k = new_v = None
    out = None
    for i in range(DEPTH):
        kind, j = i % 3, i // 3
        g1 = norm1_g[i].reshape(1, D)
        if kind == 0:
            consts = (g1, pool_w[j].astype(bf16), pool_scale[j].reshape(1, D))
            x_all = _seq_mixer_call(_pool_kernel, "pool_mix", x_all, mod4, i, consts)
        elif kind == 1:
            wqkv = attn_wqkv[j].astype(bf16)
            q_c, new_k, new_v = _qkv_call(x_all, mod4, i, g1, wqkv, ctx=True)
            q_l, k_l, v_l = _qkv_call(x_all, mod4, i, g1, wqkv, ctx=False)
            o_c = _attn_ctx_call(q_c, new_k, new_v)
            o_l = _attn_lat_call(q_l, k_l, v_l,
                                 cache_k[:, j].reshape(DEC_BATCH, PAST_LEN, D),
                                 cache_v[:, j].reshape(DEC_BATCH, PAST_LEN, D),
                                 _rel_bias_table(attn_rpb[j]))
            o_all = jnp.concatenate([o_c, o_l], axis=0)
            x_all = _proj_call(o_all, x_all, mod4, i, attn_wo[j].astype(bf16))
        else:
            consts = (g1,) + _fourier_consts() + (fourier_w[j].astype(bf16),)
            x_all = _seq_mixer_call(_fourier_kernel, "fourier_mix", x_all, mod4, i, consts)
        fg = final_norm_g.reshape(1, D) if i == DEPTH - 1 else None
        out = _moe_layer(x_all, mod4, i, norm2_g[i].reshape(1, D), rwt, rb,
                         moe_w_gate, moe_w_up, moe_w_down, fg)
        if fg is None:
            x_all = out
    y_c, y_l = out
    return (y_c.reshape(BATCH, SEQ, D), y_l.reshape(DEC_BATCH, DEC_SEQ, D),
            new_k.reshape(BATCH, 1, SEQ, N_HEADS, HEAD_DIM),
            new_v.reshape(BATCH, 1, SEQ, N_HEADS, HEAD_DIM))
```
